```python
import math
import jax, jax.numpy as jnp
from jax import lax
import numpy as np

D_MODEL = 4096
BATCH = 4
SEQ = 2048
DEPTH = 1
DEC_BATCH = 128
DEC_SEQ = 8
PAST_LEN = 16384
PAGE_SIZE = 128

N_META = 16
SSM_WIDTH = D_MODEL // 2
SSM_GROUP = 16
SSM_GROUPS = SSM_WIDTH // SSM_GROUP
SSM_STATE = 64
CONV_WIDTH = D_MODEL // 2
CONV_K = 3
D_FF = ((8 * D_MODEL + 3 * 256 - 1) // (3 * 256)) * 256
IN_COLS = SSM_WIDTH + 3 * CONV_WIDTH + 2 * D_MODEL
DT_MIN = 0.001
DT_MAX = 0.1
EPS = 1e-6

kernel_name = 'hybrid_s5_shortconv_decode_step'


def rmsnorm(x, g):
    x32 = x.astype(jnp.float32)
    y = x32 * lax.rsqrt(jnp.mean(x32 * x32, axis=-1, keepdims=True) + EPS)
    return (y * g.astype(jnp.float32)).astype(x.dtype)


def _scan_op(left, right):
    a1, b1 = left
    a2, b2 = right
    return a1 * a2, a2 * b1 + b2


def s5_mixer(u, s0_re, s0_im, lam_re, lam_im, log_dt, b_re, b_im, c_re, c_im, d_skip):
    f32 = jnp.float32
    n, l, _ = u.shape
    u32 = u.astype(f32).reshape(n, l, SSM_GROUPS, SSM_GROUP)
    lam = lax.complex(lam_re.astype(f32), lam_im.astype(f32))
    dt = jnp.exp(log_dt.astype(f32))[:, None]
    lam_bar = jnp.exp(lam * dt)
    b_bar = ((lam_bar - 1.0) / lam)[:, :, None] * lax.complex(b_re.astype(f32), b_im.astype(f32))
    bu = jnp.einsum('nlgh,gph->nlgp', u32.astype(jnp.complex64), b_bar)
    s0 = lax.complex(s0_re.astype(f32), s0_im.astype(f32))
    bu = bu.at[:, 0].add(lam_bar * s0)
    a = jnp.broadcast_to(lam_bar, bu.shape)
    _, s = lax.associative_scan(_scan_op, (a, bu), axis=1)
    c = lax.complex(c_re.astype(f32), c_im.astype(f32))
    y = jnp.real(jnp.einsum('nlgp,ghp->nlgh', s, c)) + d_skip.astype(f32).reshape(SSM_GROUPS, SSM_GROUP) * u32
    s_last = s[:, -1]
    return y.reshape(n, l, SSM_WIDTH), jnp.real(s_last), jnp.imag(s_last)


def short_conv_mixer(h, gate_b, gate_c, conv_state, conv_w):
    l = h.shape[1]
    z = gate_c * h
    zp = jnp.concatenate([conv_state.astype(z.dtype), z], axis=1)
    y = conv_w[0] * zp[:, 0:l]
    for k in range(1, CONV_K):
        y = y + conv_w[k] * zp[:, k:k + l]
    return gate_b * y, zp[:, -(CONV_K - 1):]


def trunk_layer(x, s_re, s_im, conv_state, g_pre_mix, w_in, lam_re, lam_im, log_dt, b_re, b_im,
                c_re, c_im, ssm_d, w_glu_v, w_glu_g, conv_w, w_conv_out, w_o, g_post_mix,
                g_pre_ffn, w_ffn_gate, w_ffn_up, w_ffn_down, g_post_ffn):
    hn = rmsnorm(x, g_pre_mix)
    proj = hn @ w_in
    o1 = SSM_WIDTH
    o2 = o1 + CONV_WIDTH
    o3 = o2 + CONV_WIDTH
    o4 = o3 + CONV_WIDTH
    o5 = o4 + D_MODEL
    u = proj[..., :o1]
    ch = proj[..., o1:o2]
    cb = proj[..., o2:o3]
    cc = proj[..., o3:o4]
    ga = proj[..., o4:o5]
    gb = proj[..., o5:]
    y_ssm, ns_re, ns_im = s5_mixer(u, s_re, s_im, lam_re, lam_im, log_dt, b_re, b_im, c_re, c_im, ssm_d)
    y_ssm = jax.nn.gelu(y_ssm).astype(x.dtype)
    y_a = (y_ssm @ w_glu_v) * jax.nn.sigmoid(y_ssm @ w_glu_g)
    y_conv, n_conv = short_conv_mixer(ch, cb, cc, conv_state, conv_w)
    y_b = y_conv @ w_conv_out
    merged = jax.nn.sigmoid(ga) * y_a + jax.nn.sigmoid(gb) * y_b
    x = x + rmsnorm(merged @ w_o, g_post_mix)
    hf = rmsnorm(x, g_pre_ffn)
    f = (jax.nn.silu(hf @ w_ffn_gate) * (hf @ w_ffn_up)) @ w_ffn_down
    x = x + rmsnorm(f, g_post_ffn)
    return x, ns_re, ns_im, n_conv


def setup_inputs(seed: int = 0) -> dict:
    key = jax.random.key(seed)
    ks = jax.random.split(key, 32)
    f32 = jnp.float32

    def nrm(k, shape, scale):
        return jax.random.normal(k, shape, f32) * scale

    def gain(k, shape):
        return 1.0 + 0.02 * jax.random.normal(k, shape, f32)

    n_idx = jnp.arange(SSM_STATE, dtype=f32)
    lam_im = jnp.broadcast_to(math.pi * n_idx, (DEPTH, SSM_GROUPS, SSM_STATE)) + nrm(ks[8], (DEPTH, SSM_GROUPS, SSM_STATE), 0.01)
    return {
        'x_prompt': nrm(ks[0], (BATCH, SEQ, D_MODEL), 1.0),
        'x_sample': nrm(ks[1], (DEC_BATCH, DEC_SEQ, D_MODEL), 1.0),
        'state_ssm_re': nrm(ks[2], (DEPTH, DEC_BATCH, SSM_GROUPS, SSM_STATE), 0.1),
        'state_ssm_im': nrm(ks[3], (DEPTH, DEC_BATCH, SSM_GROUPS, SSM_STATE), 0.1),
        'state_conv': nrm(ks[4], (DEPTH, DEC_BATCH, CONV_K - 1, CONV_WIDTH), 1.0),
        'meta_tokens': nrm(ks[5], (N_META, D_MODEL), 1.0),
        'g_pre_mix': gain(ks[6], (DEPTH, D_MODEL)),
        'w_in': nrm(ks[7], (DEPTH, D_MODEL, IN_COLS), D_MODEL ** -0.5),
        'ssm_lambda_re': -0.5 + nrm(ks[9], (DEPTH, SSM_GROUPS, SSM_STATE), 0.01),
        'ssm_lambda_im': lam_im,
        'ssm_log_dt': jax.random.uniform(ks[10], (DEPTH, SSM_GROUPS), f32, math.log(DT_MIN), math.log(DT_MAX)),
        'ssm_b_re': nrm(ks[11], (DEPTH, SSM_GROUPS, SSM_STATE, SSM_GROUP), (2 * SSM_GROUP) ** -0.5),
        'ssm_b_im': nrm(ks[12], (DEPTH, SSM_GROUPS, SSM_STATE, SSM_GROUP), (2 * SSM_GROUP) ** -0.5),
        'ssm_c_re': nrm(ks[13], (DEPTH, SSM_GROUPS, SSM_GROUP, SSM_STATE), (2 * SSM_STATE) ** -0.5),
        'ssm_c_im': nrm(ks[14], (DEPTH, SSM_GROUPS, SSM_GROUP, SSM_STATE), (2 * SSM_STATE) ** -0.5),
        'ssm_d': nrm(ks[15], (DEPTH, SSM_WIDTH), 1.0),
        'w_glu_v': nrm(ks[16], (DEPTH, SSM_WIDTH, D_MODEL), SSM_WIDTH ** -0.5),
        'w_glu_g': nrm(ks[17], (DEPTH, SSM_WIDTH, D_MODEL), SSM_WIDTH ** -0.5),
        'conv_w': nrm(ks[18], (DEPTH, CONV_K, CONV_WIDTH), CONV_K ** -0.5),
        'w_conv_out': nrm(ks[19], (DEPTH, CONV_WIDTH, D_MODEL), CONV_WIDTH ** -0.5),
        'w_o': nrm(ks[20], (DEPTH, D_MODEL, D_MODEL), D_MODEL ** -0.5),
        'g_post_mix': gain(ks[21], (DEPTH, D_MODEL)),
        'g_pre_ffn': gain(ks[22], (DEPTH, D_MODEL)),
        'w_ffn_gate': nrm(ks[23], (DEPTH, D_MODEL, D_FF), D_MODEL ** -0.5),
        'w_ffn_up': nrm(ks[24], (DEPTH, D_MODEL, D_FF), D_MODEL ** -0.5),
        'w_ffn_down': nrm(ks[25], (DEPTH, D_FF, D_MODEL), D_FF ** -0.5),
        'g_post_ffn': gain(ks[26], (DEPTH, D_MODEL)),
    }


def reference(x_prompt, x_sample, state_ssm_re, state_ssm_im, state_conv, meta_tokens, g_pre_mix, w_in,
              ssm_lambda_re, ssm_lambda_im, ssm_log_dt, ssm_b_re, ssm_b_im, ssm_c_re, ssm_c_im, ssm_d,
              w_glu_v, w_glu_g, conv_w, w_conv_out, w_o, g_post_mix, g_pre_ffn, w_ffn_gate, w_ffn_up,
              w_ffn_down, g_post_ffn):
    nb = x_prompt.shape[0]
    meta = jnp.broadcast_to(meta_tokens.astype(x_prompt.dtype)[None], (nb, N_META, D_MODEL))
    xp = jnp.concatenate([meta, x_prompt], axis=1)
    xs = x_sample
    zero_s = jnp.zeros((nb, SSM_GROUPS, SSM_STATE), jnp.float32)
    zero_c = jnp.zeros((nb, CONV_K - 1, CONV_WIDTH), xp.dtype)
    p_re, p_im, p_cv, s_re, s_im, s_cv = [], [], [], [], [], []
    for i in range(DEPTH):
        lw = (g_pre_mix[i], w_in[i], ssm_lambda_re[i], ssm_lambda_im[i], ssm_log_dt[i], ssm_b_re[i],
              ssm_b_im[i], ssm_c_re[i], ssm_c_im[i], ssm_d[i], w_glu_v[i], w_glu_g[i], conv_w[i],
              w_conv_out[i], w_o[i], g_post_mix[i], g_pre_ffn[i], w_ffn_gate[i], w_ffn_up[i],
              w_ffn_down[i], g_post_ffn[i])
        xp, a_re, a_im, a_cv = trunk_layer(xp, zero_s, zero_s, zero_c, *lw)
        xs, b_re, b_im, b_cv = trunk_layer(xs, state_ssm_re[i], state_ssm_im[i], state_conv[i], *lw)
        p_re.append(a_re)
        p_im.append(a_im)
        p_cv.append(a_cv)
        s_re.append(b_re)
        s_im.append(b_im)
        s_cv.append(b_cv)
    y_prompt = xp[:, N_META:]
    return (y_prompt, xs, jnp.stack(p_re), jnp.stack(p_im), jnp.stack(p_cv),
            jnp.stack(s_re), jnp.stack(s_im), jnp.stack(s_cv))
```

```python
import functools

import jax
import jax.numpy as jnp
from jax import lax
from jax.experimental import pallas as pl
from jax.experimental.pallas import tpu as pltpu

F32 = jnp.float32
BF16 = jnp.bfloat16

D_MODEL = 4096
N_META = 16
SSM_WIDTH = 2048
SSM_GROUP = 16
SSM_GROUPS = 128
SSM_STATE = 64
CONV_WIDTH = 2048
CONV_K = 3
D_FF = 11008
IN_COLS = 16384
EPS = 1e-6

RUN = 16
LANES = 128
GROUPS_PER_BLOCK = LANES // SSM_GROUP
N_BLOCKS = SSM_WIDTH // LANES
STATE_COLS = GROUPS_PER_BLOCK * SSM_STATE
VMEM_LIMIT = 56 * 1024 * 1024


def _params(semantics, limit=VMEM_LIMIT):
    return pltpu.CompilerParams(dimension_semantics=semantics, vmem_limit_bytes=limit)


def _rms(x, g):
    return x * lax.rsqrt(jnp.mean(x * x, axis=-1, keepdims=True) + EPS) * g


ROW_CHUNK = 128
COL_CHUNK = 512


def _row_chunks(n):
    return [(s, min(ROW_CHUNK, n - s)) for s in range(0, n, ROW_CHUNK)]


def _accumulate(acc_ref, lhs, w_ref, first):
    @pl.when(first)
    def _():
        acc_ref[...] = jnp.zeros_like(acc_ref)

    for c0 in range(0, acc_ref.shape[1], COL_CHUNK):
        acc_ref[:, c0:c0 + COL_CHUNK] += jnp.dot(lhs, w_ref[:, c0:c0 + COL_CHUNK],
                                                 preferred_element_type=F32)


def _prenorm_kernel(x_ref, g_ref, o_ref):
    g = g_ref[...]
    for r0, nr in _row_chunks(x_ref.shape[0]):
        o_ref[r0:r0 + nr, :] = _rms(x_ref[r0:r0 + nr, :], g).astype(BF16)


def _inproj_kernel(h_ref, w_ref, o_ref, wb_ref):
    @pl.when(pl.program_id(1) == 0)
    def _():
        wb_ref[...] = w_ref[...].astype(BF16)

    o_ref[...] = jnp.dot(h_ref[...], wb_ref[...], preferred_element_type=F32).astype(BF16)


def _ssm_kernel(n_runs, n_srow, n_seq, n_chunk,
                u_ref, dk_ref, f_ref, e_ref, a16_ref, a8_ref, s0_ref,
                y_ref, so_ref, po_ref, w_scr, v_scr, s_scr):
    half = RUN // 2
    hw = half * LANES
    sc = STATE_COLS

    zero = jnp.zeros((LANES, LANES), BF16)
    for tau in range(RUN):
        for t in range(RUN):
            blk = dk_ref[t - tau] if t >= tau else zero
            w_scr[tau * LANES:(tau + 1) * LANES, t * LANES:(t + 1) * LANES] = blk

    lhs = jnp.concatenate([u_ref[t] for t in range(RUN)], axis=1)

    lp = lhs[n_srow:]
    v_scr[...] = jnp.dot(lp, f_ref[...], preferred_element_type=F32)
    a_re = a16_ref[:, :sc]
    a_im = a16_ref[:, sc:]

    def body(c, carry):
        out = []
        for n in range(n_seq):
            s_re, s_im = carry[n]
            row = n * n_chunk + c
            s_scr[pl.ds(row, 1), :sc] = s_re
            s_scr[pl.ds(row, 1), sc:] = s_im
            v = v_scr[pl.ds(row, 1), :]
            out.append((a_re * s_re - a_im * s_im + v[:, :sc],
                        a_re * s_im + a_im * s_re + v[:, sc:]))
        return tuple(out)

    z = jnp.zeros((1, sc), F32)
    fin = lax.fori_loop(0, n_chunk, body, tuple((z, z) for _ in range(n_seq)))
    for n in range(n_seq):
        po_ref[n:n + 1, :sc] = fin[n][0]
        po_ref[n:n + 1, sc:] = fin[n][1]

    yp = jnp.dot(lp, w_scr[...], preferred_element_type=F32)
    yp = yp + jnp.dot(s_scr[...].astype(BF16), e_ref[...], preferred_element_type=F32)
    for t in range(RUN):
        y_ref[t, n_srow:, :] = jax.nn.gelu(yp[:, t * LANES:(t + 1) * LANES]).astype(BF16)

    ls = lhs[:n_srow]
    ls2 = jnp.concatenate([ls[:, :hw], ls[:, hw:]], axis=0)
    s0 = jnp.concatenate([s0_ref[0], s0_ref[1]], axis=0)
    ys = jnp.dot(ls2, w_scr[:hw, :hw], preferred_element_type=F32)
    ys = ys + jnp.dot(s0.astype(BF16), e_ref[:, :hw], preferred_element_type=F32)
    vs = jnp.dot(ls2, f_ref[hw:, :], preferred_element_type=F32)
    b_re = a8_ref[:, :sc]
    b_im = a8_ref[:, sc:]
    s0_re = s0[:, :sc]
    s0_im = s0[:, sc:]
    n_re = b_re * s0_re - b_im * s0_im + vs[:, :sc]
    n_im = b_re * s0_im + b_im * s0_re + vs[:, sc:]
    for h in range(2):
        so_ref[h, :, :sc] = n_re[h * n_srow:(h + 1) * n_srow]
        so_ref[h, :, sc:] = n_im[h * n_srow:(h + 1) * n_srow]
        for t in range(half):
            blk = ys[h * n_srow:(h + 1) * n_srow, t * LANES:(t + 1) * LANES]
            y_ref[t + half * h, :n_srow, :] = jax.nn.gelu(blk).astype(BF16)


def _conv_kernel(n_runs, n_srow, n_seq, n_chunk,
                 ch_ref, cb_ref, cc_ref, w_ref, cs_ref, y_ref, so_ref, po_ref, zs_ref):
    pad = 8
    n_prow = n_runs - n_srow
    w0 = w_ref[0:1, :]
    w1 = w_ref[1:2, :]
    w2 = w_ref[2:3, :]

    def gated(t):
        return cc_ref[t].astype(F32) * ch_ref[t].astype(F32)

    z14 = gated(RUN - 2)
    z15 = gated(RUN - 1)
    zs_ref[0, pad:pad + n_runs, :] = z14
    zs_ref[1, pad:pad + n_runs, :] = z15
    ridx = lax.broadcasted_iota(jnp.int32, (n_prow, LANES), 0)
    first = ridx == 0
    for n in range(1, n_seq):
        first = first | (ridx == n * n_chunk)
    lo = pad + n_srow - 1
    p2 = jnp.where(first, 0.0, zs_ref[0, lo:lo + n_prow, :])
    p1 = jnp.where(first, 0.0, zs_ref[1, lo:lo + n_prow, :])

    half = RUN // 2
    zm2 = zm1 = None
    for t in range(RUN):
        zt = gated(t)
        ztp, zts = zt[n_srow:], zt[:n_srow]
        if t == 0:
            a1, a2 = p1, p2
        elif t == 1:
            a1, a2 = zm1[n_srow:], p1
        else:
            a1, a2 = zm1[n_srow:], zm2[n_srow:]
        yp = cb_ref[t, n_srow:, :].astype(F32) * (w0 * a2 + w1 * a1 + w2 * ztp)
        y_ref[t, n_srow:, :] = yp.astype(BF16)
        h, tt = divmod(t, half)
        if tt == 0:
            b1, b2 = cs_ref[h, 1], cs_ref[h, 0]
        elif tt == 1:
            b1, b2 = zm1[:n_srow], cs_ref[h, 1]
        else:
            b1, b2 = zm1[:n_srow], zm2[:n_srow]
        ysm = cb_ref[t, :n_srow, :].astype(F32) * (w0 * b2 + w1 * b1 + w2 * zts)
        y_ref[t, :n_srow, :] = ysm.astype(BF16)
        if tt >= half - 2:
            so_ref[h, tt - (half - 2)] = zts
        zm2, zm1 = zm1, zt

    for k in range(2):
        for n in range(n_seq):
            row = pad + n_srow + n * n_chunk + n_chunk - 1
            po_ref[k, n:n + 1, :] = zs_ref[k, row:row + 1, :]


def _merge_kernel(ys_ref, yc_ref, ga_ref, gb_ref, wv_ref, wg_ref, wc_ref, o_ref,
                  wvb_ref, wgb_ref, wcb_ref):
    @pl.when(pl.program_id(1) == 0)
    def _():
        wvb_ref[...] = wv_ref[...].astype(BF16)
        wgb_ref[...] = wg_ref[...].astype(BF16)
        wcb_ref[...] = wc_ref[...].astype(BF16)

    ys = ys_ref[...]
    y_a = (jnp.dot(ys, wvb_ref[...], preferred_element_type=F32)
           * jax.nn.sigmoid(jnp.dot(ys, wgb_ref[...], preferred_element_type=F32)))
    y_b = jnp.dot(yc_ref[...], wcb_ref[...], preferred_element_type=F32)
    m = (jax.nn.sigmoid(ga_ref[...].astype(F32)) * y_a
         + jax.nn.sigmoid(gb_ref[...].astype(F32)) * y_b)
    o_ref[...] = m.astype(BF16)


def _oproj_kernel(m_ref, w_ref, x_ref, g1_ref, g2_ref, x1_ref, hf_ref):
    k = pl.program_id(1)
    _accumulate(x1_ref, m_ref[...], w_ref, k == 0)

    @pl.when(k == pl.num_programs(1) - 1)
    def _():
        g1 = g1_ref[...]
        g2 = g2_ref[...]
        for r0, nr in _row_chunks(x1_ref.shape[0]):
            x1 = x_ref[r0:r0 + nr, :] + _rms(x1_ref[r0:r0 + nr, :], g1)
            x1_ref[r0:r0 + nr, :] = x1
            hf_ref[r0:r0 + nr, :] = _rms(x1, g2).astype(BF16)


def _ffn_kernel(hf_ref, wg_ref, wu_ref, wd_ref, x1_ref, g_ref, o_ref):
    j = pl.program_id(1)
    hf = hf_ref[...]
    gate = jnp.dot(hf, wg_ref[...], preferred_element_type=F32)
    up = jnp.dot(hf, wu_ref[...], preferred_element_type=F32)
    h = (jax.nn.silu(gate) * up).astype(BF16)
    _accumulate(o_ref, h, wd_ref, j == 0)

    @pl.when(j == pl.num_programs(1) - 1)
    def _():
        g = g_ref[...]
        for r0, nr in _row_chunks(o_ref.shape[0]):
            o_ref[r0:r0 + nr, :] = x1_ref[r0:r0 + nr, :] + _rms(o_ref[r0:r0 + nr, :], g)


def _ssm_tables(lam_re, lam_im, log_dt, b_re, b_im, c_re, c_im, d_skip):
    g, p, hh = SSM_GROUPS, SSM_STATE, SSM_GROUP
    gb, nb = GROUPS_PER_BLOCK, N_BLOCKS
    lam = lax.complex(lam_re, lam_im)
    dt = jnp.exp(log_dt)[:, None]
    ldt = lam * dt
    lam_bar = jnp.exp(ldt)
    b_bar = ((lam_bar - 1.0) / lam)[:, :, None] * lax.complex(b_re, b_im)
    c = lax.complex(c_re, c_im)
    ks = jnp.arange(RUN + 1, dtype=F32)
    pw = jnp.exp(ks[:, None, None] * ldt[None])
    kk = jnp.real(jnp.einsum('gop,kgp,gpi->kgoi', c, pw[:RUN], b_bar,
                             precision=lax.Precision.HIGHEST))
    kk = kk.at[0].add(jnp.eye(hh, dtype=F32)[None] * d_skip.reshape(g, hh)[:, :, None])
    eye = jnp.eye(gb, dtype=F32)
    kb = kk.reshape(RUN, nb, gb, hh, hh)
    dk = jnp.einsum('kGaoi,ab->Gkaibo', kb, eye).reshape(nb, RUN, LANES, LANES)
    fb = pw[RUN - 1 - jnp.arange(RUN)][:, :, :, None] * b_bar[None]
    fb = fb.reshape(RUN, nb, gb, p, hh)
    f_re = jnp.einsum('tGapi,ab->Gtaibp', jnp.real(fb), eye)
    f_im = jnp.einsum('tGapi,ab->Gtaibp', jnp.imag(fb), eye)
    f = jnp.stack([f_re, f_im], axis=4)
    f = f.reshape(nb, RUN * LANES, 2 * STATE_COLS)
    cl = c[None] * pw[1:RUN + 1][:, :, None, :]
    cl = cl.reshape(RUN, nb, gb, hh, p)
    e_re = jnp.einsum('tGaop,ab->Gaptbo', jnp.real(cl), eye)
    e_im = jnp.einsum('tGaop,ab->Gaptbo', -jnp.imag(cl), eye)
    e = jnp.stack([e_re, e_im], axis=1).reshape(nb, 2 * STATE_COLS, RUN * LANES)

    def decay(k):
        a = pw[k].reshape(nb, 1, STATE_COLS)
        return jnp.concatenate([jnp.real(a), jnp.imag(a)], axis=-1)

    return dk.astype(BF16), f.astype(BF16), e.astype(BF16), decay(RUN), decay(RUN // 2)


def kernel(x_prompt, x_sample, state_ssm_re, state_ssm_im, state_conv, meta_tokens, g_pre_mix, w_in,
           ssm_lambda_re, ssm_lambda_im, ssm_log_dt, ssm_b_re, ssm_b_im, ssm_c_re, ssm_c_im, ssm_d,
           w_glu_v, w_glu_g, conv_w, w_conv_out, w_o, g_post_mix, g_pre_ffn, w_ffn_gate, w_ffn_up,
           w_ffn_down, g_post_ffn):
    nb_, seq, d = x_prompt.shape
    dec_b, dec_seq, _ = x_sample.shape
    assert d == D_MODEL and dec_seq == RUN // 2 and dec_b % 2 == 0
    assert (seq + N_META) % RUN == 0 and w_in.shape[0] == 1
    n_chunk = (seq + N_META) // RUN
    n_srow = dec_b // 2
    n_runs = n_srow + nb_ * n_chunk
    n_tok = n_runs * RUN
    assert n_srow % 16 == 0

    meta = jnp.broadcast_to(meta_tokens.astype(x_prompt.dtype)[None], (nb_, N_META, d))
    xcat = jnp.concatenate([x_sample.reshape(dec_b * dec_seq, d),
                            jnp.concatenate([meta, x_prompt], axis=1).reshape(nb_ * (seq + N_META), d)],
                           axis=0)
    x_view = xcat.reshape(n_runs, RUN * d)

    row = lambda a: a.reshape(1, -1)
    plane = lambda c: (None, n_runs, c)

    hn = pl.pallas_call(
        _prenorm_kernel,
        out_shape=jax.ShapeDtypeStruct((RUN, n_runs, d), BF16),
        grid=(RUN,),
        in_specs=[pl.BlockSpec((n_runs, d), lambda t: (0, t)),
                  pl.BlockSpec((1, d), lambda t: (0, 0))],
        out_specs=pl.BlockSpec(plane(d), lambda t: (t, 0, 0)),
        compiler_params=_params(("arbitrary",)),
        name="prenorm",
    )(x_view, row(g_pre_mix))

    tn = 512
    proj = pl.pallas_call(
        _inproj_kernel,
        out_shape=jax.ShapeDtypeStruct((RUN, n_runs, IN_COLS), BF16),
        grid=(IN_COLS // tn, RUN),
        in_specs=[pl.BlockSpec(plane(d), lambda j, t: (t, 0, 0)),
                  pl.BlockSpec((d, tn), lambda j, t: (0, j))],
        out_specs=pl.BlockSpec(plane(tn), lambda j, t: (t, 0, j)),
        scratch_shapes=[pltpu.VMEM((d, tn), BF16)],
        compiler_params=_params(("arbitrary", "arbitrary")),
        name="inproj",
    )(hn, w_in.reshape(d, IN_COLS))

    dk, f_tab, e_tab, a16, a8 = _ssm_tables(
        ssm_lambda_re[0], ssm_lambda_im[0], ssm_log_dt[0], ssm_b_re[0], ssm_b_im[0],
        ssm_c_re[0], ssm_c_im[0], ssm_d[0])

    def state_in(s):
        s = s.reshape(n_srow, 2, N_BLOCKS, STATE_COLS)
        return jnp.transpose(s, (2, 1, 0, 3))

    s0 = jnp.concatenate([state_in(state_ssm_re), state_in(state_ssm_im)], axis=-1)
    sc2 = 2 * STATE_COLS
    n_prow = n_runs - n_srow
    y_ssm, s_out, p_out = pl.pallas_call(
        functools.partial(_ssm_kernel, n_runs, n_srow, nb_, n_chunk),
        out_shape=(jax.ShapeDtypeStruct((RUN, n_runs, SSM_WIDTH), BF16),
                   jax.ShapeDtypeStruct((N_BLOCKS, 2, n_srow, sc2), F32),
                   jax.ShapeDtypeStruct((N_BLOCKS, nb_, sc2), F32)),
        grid=(N_BLOCKS,),
        in_specs=[pl.BlockSpec((RUN, n_runs, LANES), lambda g: (0, 0, g)),
                  pl.BlockSpec((None, RUN, LANES, LANES), lambda g: (g, 0, 0, 0)),
                  pl.BlockSpec((None, RUN * LANES, sc2), lambda g: (g, 0, 0)),
                  pl.BlockSpec((None, sc2, RUN * LANES), lambda g: (g, 0, 0)),
                  pl.BlockSpec((None, 1, sc2), lambda g: (g, 0, 0)),
                  pl.BlockSpec((None, 1, sc2), lambda g: (g, 0, 0)),
                  pl.BlockSpec((None, 2, n_srow, sc2), lambda g: (g, 0, 0, 0))],
        out_specs=(pl.BlockSpec((RUN, n_runs, LANES), lambda g: (0, 0, g)),
                   pl.BlockSpec((None, 2, n_srow, sc2), lambda g: (g, 0, 0, 0)),
                   pl.BlockSpec((None, nb_, sc2), lambda g: (g, 0, 0))),
        scratch_shapes=[pltpu.VMEM((RUN * LANES, RUN * LANES), BF16),
                        pltpu.VMEM((n_prow, sc2), F32),
                        pltpu.VMEM((n_prow, sc2), F32)],
        compiler_params=_params(("arbitrary",)),
        name="s5",
    )(proj, dk, f_tab, e_tab, a16, a8, s0)

    cblk = SSM_WIDTH // LANES
    cs = jnp.transpose(state_conv.reshape(n_srow, 2, CONV_K - 1, CONV_WIDTH), (1, 2, 0, 3))
    y_conv, cs_out, cp_out = pl.pallas_call(
        functools.partial(_conv_kernel, n_runs, n_srow, nb_, n_chunk),
        out_shape=(jax.ShapeDtypeStruct((RUN, n_runs, CONV_WIDTH), BF16),
                   jax.ShapeDtypeStruct((2, CONV_K - 1, n_srow, CONV_WIDTH), F32),
                   jax.ShapeDtypeStruct((CONV_K - 1, nb_, CONV_WIDTH), F32)),
        grid=(CONV_WIDTH // LANES,),
        in_specs=[pl.BlockSpec((RUN, n_runs, LANES), lambda j: (0, 0, cblk + j)),
                  pl.BlockSpec((RUN, n_runs, LANES), lambda j: (0, 0, 2 * cblk + j)),
                  pl.BlockSpec((RUN, n_runs, LANES), lambda j: (0, 0, 3 * cblk + j)),
                  pl.BlockSpec((CONV_K, LANES), lambda j: (0, j)),
                  pl.BlockSpec((2, CONV_K - 1, n_srow, LANES), lambda j: (0, 0, 0, j))],
        out_specs=(pl.BlockSpec((RUN, n_runs, LANES), lambda j: (0, 0, j)),
                   pl.BlockSpec((2, CONV_K - 1, n_srow, LANES), lambda j: (0, 0, 0, j)),
                   pl.BlockSpec((CONV_K - 1, nb_, LANES), lambda j: (0, 0, j))),
        scratch_shapes=[pltpu.VMEM((2, n_runs + 16, LANES), F32)],
        compiler_params=_params(("arbitrary",)),
        name="shortconv",
    )(proj, proj, proj, conv_w.reshape(CONV_K, CONV_WIDTH), cs)

    tn3 = 512
    ga_blk = (SSM_WIDTH + 3 * CONV_WIDTH) // tn3
    gb_blk = ga_blk + d // tn3
    merged = pl.pallas_call(
        _merge_kernel,
        out_shape=jax.ShapeDtypeStruct((RUN, n_runs, d), BF16),
        grid=(d // tn3, RUN),
        in_specs=[pl.BlockSpec(plane(SSM_WIDTH), lambda j, t: (t, 0, 0)),
                  pl.BlockSpec(plane(CONV_WIDTH), lambda j, t: (t, 0, 0)),
                  pl.BlockSpec(plane(tn3), lambda j, t: (t, 0, ga_blk + j)),
                  pl.BlockSpec(plane(tn3), lambda j, t: (t, 0, gb_blk + j)),
                  pl.BlockSpec((SSM_WIDTH, tn3), lambda j, t: (0, j)),
                  pl.BlockSpec((SSM_WIDTH, tn3), lambda j, t: (0, j)),
                  pl.BlockSpec((CONV_WIDTH, tn3), lambda j, t: (0, j))],
        out_specs=pl.BlockSpec(plane(tn3), lambda j, t: (t, 0, j)),
        scratch_shapes=[pltpu.VMEM((SSM_WIDTH, tn3), BF16),
                        pltpu.VMEM((SSM_WIDTH, tn3), BF16),
                        pltpu.VMEM((CONV_WIDTH, tn3), BF16)],
        compiler_params=_params(("arbitrary", "arbitrary")),
        name="merge",
    )(y_ssm, y_conv, proj, proj, w_glu_v.reshape(SSM_WIDTH, d), w_glu_g.reshape(SSM_WIDTH, d),
      w_conv_out.reshape(CONV_WIDTH, d))

    tk = 512
    x1, hf = pl.pallas_call(
        _oproj_kernel,
        out_shape=(jax.ShapeDtypeStruct((RUN, n_runs, d), F32),
                   jax.ShapeDtypeStruct((RUN, n_runs, d), BF16)),
        grid=(RUN, d // tk),
        in_specs=[pl.BlockSpec(plane(tk), lambda t, k: (t, 0, k)),
                  pl.BlockSpec((tk, d), lambda t, k: (k, 0)),
                  pl.BlockSpec((n_runs, d), lambda t, k: (0, t), pipeline_mode=pl.Buffered(1)),
                  pl.BlockSpec((1, d), lambda t, k: (0, 0)),
                  pl.BlockSpec((1, d), lambda t, k: (0, 0))],
        out_specs=(pl.BlockSpec(plane(d), lambda t, k: (t, 0, 0)),
                   pl.BlockSpec(plane(d), lambda t, k: (t, 0, 0))),
        compiler_params=_params(("arbitrary", "arbitrary")),
        name="oproj",
    )(merged, w_o.reshape(d, d).astype(BF16), x_view, row(g_post_mix), row(g_pre_ffn))

    tf = 256
    y_view = pl.pallas_call(
        _ffn_kernel,
        out_shape=jax.ShapeDtypeStruct((n_runs, RUN * d), F32),
        grid=(RUN, D_FF // tf),
        in_specs=[pl.BlockSpec(plane(d), lambda t, j: (t, 0, 0)),
                  pl.BlockSpec((d, tf), lambda t, j: (0, j)),
                  pl.BlockSpec((d, tf), lambda t, j: (0, j)),
                  pl.BlockSpec((tf, d), lambda t, j: (j, 0)),
                  pl.BlockSpec(plane(d), lambda t, j: (t, 0, 0), pipeline_mode=pl.Buffered(1)),
                  pl.BlockSpec((1, d), lambda t, j: (0, 0))],
        out_specs=pl.BlockSpec((n_runs, d), lambda t, j: (0, t)),
        compiler_params=_params(("arbitrary", "arbitrary")),
        name="ffn",
    )(hf, w_ffn_gate.reshape(d, D_FF).astype(BF16), w_ffn_up.reshape(d, D_FF).astype(BF16),
      w_ffn_down.reshape(D_FF, d).astype(BF16), x1, row(g_post_ffn))

    y = y_view.reshape(n_tok, d)
    n_stok = dec_b * dec_seq
    y_sample = y[:n_stok].reshape(dec_b, dec_seq, d)
    y_prompt = y[n_stok:].reshape(nb_, seq + N_META, d)[:, N_META:]

    def state_out(s):
        return jnp.transpose(s, (2, 1, 0, 3)).reshape(1, dec_b, SSM_GROUPS, SSM_STATE)

    def prompt_state(s):
        return jnp.transpose(s, (1, 0, 2)).reshape(1, nb_, SSM_GROUPS, SSM_STATE)

    p_re = prompt_state(p_out[:, :, :STATE_COLS])
    p_im = prompt_state(p_out[:, :, STATE_COLS:])
    p_cv = jnp.transpose(cp_out, (1, 0, 2))[None]
    s_re = state_out(s_out[..., :STATE_COLS])
    s_im = state_out(s_out[..., STATE_COLS:])
    s_cv = jnp.transpose(cs_out, (2, 0, 1, 3)).reshape(1, dec_b, CONV_K - 1, CONV_WIDTH)
    return (y_prompt, y_sample, p_re, p_im, p_cv, s_re, s_im, s_cv)
```

```python
import functools

import jax
import jax.numpy as jnp
from jax import lax
from jax.experimental import pallas as pl
from jax.experimental.pallas import tpu as pltpu

F32 = jnp.float32
BF16 = jnp.bfloat16

D_MODEL = 4096
N_META = 16
SSM_WIDTH = 2048
SSM_GROUP = 16
SSM_GROUPS = 128
SSM_STATE = 64
CONV_WIDTH = 2048
CONV_K = 3
D_FF = 11008
IN_COLS = 16384
EPS = 1e-6

RUN = 16
LANES = 128
GROUPS_PER_BLOCK = LANES // SSM_GROUP
PAIRS = GROUPS_PER_BLOCK // 2
N_BLOCKS = SSM_WIDTH // LANES
STATE_COLS = GROUPS_PER_BLOCK * SSM_STATE
VMEM_LIMIT = 56 * 1024 * 1024
ROW_CHUNK = 128
COL_CHUNK = 512


def _params(semantics, limit=VMEM_LIMIT):
    return pltpu.CompilerParams(dimension_semantics=semantics, vmem_limit_bytes=limit)


def _rms(x, g):
    return x * lax.rsqrt(jnp.mean(x * x, axis=-1, keepdims=True) + EPS) * g


def _row_chunks(n):
    return [(s, min(ROW_CHUNK, n - s)) for s in range(0, n, ROW_CHUNK)]


def _cmul(ar, ai, br, bi):
    return ar * br - ai * bi, ar * bi + ai * br


def _prenorm_kernel(xs_ref, xp_ref, meta_ref, g_ref, o_ref):
    g = g_ref[...]
    ns, npr = xs_ref.shape[0], xp_ref.shape[0]
    o_ref[0:ns, :] = _rms(xs_ref[...], g).astype(BF16)
    for r0, nr in _row_chunks(npr):
        o_ref[ns + r0:ns + r0 + nr, :] = _rms(xp_ref[r0:r0 + nr, :], g).astype(BF16)
    m = _rms(meta_ref[pl.ds(pl.program_id(0), 1), :], g)
    n_meta_rows = o_ref.shape[0] - ns - npr
    o_ref[ns + npr:, :] = jnp.broadcast_to(m, (n_meta_rows, m.shape[1])).astype(BF16)


def _inproj_kernel(h_ref, w_ref, o_ref, wb_ref):
    @pl.when(pl.program_id(1) == 0)
    def _():
        wb_ref[...] = w_ref[...].astype(BF16)

    o_ref[...] = jnp.dot(h_ref[...], wb_ref[...], preferred_element_type=F32).astype(BF16)


def _lam_bar(p_ref):
    dt = jnp.exp(p_ref[2])
    mag = jnp.exp(p_ref[0] * dt)
    ang = p_ref[1] * dt
    return mag * jnp.cos(ang), mag * jnp.sin(ang)


def _ssm_kernel(n_srow, n_seq, n_rpp,
                u_ref, fs_ref, es_ref, d_ref, sr0_ref, sr1_ref, si0_ref, si1_ref,
                y_ref, sre_ref, sim_ref, pre_ref, pim_ref,
                f_scr, e_scr, w_scr, v_scr, s_scr):
    half = RUN // 2
    hw = half * LANES
    sc = STATE_COLS
    shape = (LANES, LANES)
    rid = lax.broadcasted_iota(jnp.int32, shape, 0)
    lid = lax.broadcasted_iota(jnp.int32, shape, 1)

    lr, li = _lam_bar(fs_ref)
    a, b = fs_ref[0], fs_ref[1]
    inv = 1.0 / (a * a + b * b)
    cr = ((lr - 1.0) * a + li * b) * inv
    ci = (li * a - (lr - 1.0) * b) * inv
    bbr, bbi = _cmul(cr, ci, fs_ref[3], fs_ref[4])
    own_f = [(rid >> 4) == 2 * pr + (lid >> 6) for pr in range(PAIRS)]
    pr_, pi_ = jnp.ones(shape, F32), jnp.zeros(shape, F32)
    p8 = None
    for k in range(RUN):
        if k == half:
            p8 = (pr_, pi_)
        fr, fi = _cmul(pr_, pi_, bbr, bbi)
        r0 = (RUN - 1 - k) * LANES
        for pr in range(PAIRS):
            f_scr[r0:r0 + LANES, pr * LANES:(pr + 1) * LANES] = jnp.where(own_f[pr], fr, 0.0).astype(BF16)
            f_scr[r0:r0 + LANES, (PAIRS + pr) * LANES:(PAIRS + pr + 1) * LANES] = (
                jnp.where(own_f[pr], fi, 0.0).astype(BF16))
        pr_, pi_ = _cmul(pr_, pi_, lr, li)
    p16 = (pr_, pi_)

    def decay_row(p):
        lo_half = lid[0:1, :] < SSM_STATE
        return jnp.concatenate(
            [jnp.where(lo_half, p[2 * q * SSM_GROUP:2 * q * SSM_GROUP + 1, :],
                       p[(2 * q + 1) * SSM_GROUP:(2 * q + 1) * SSM_GROUP + 1, :]) for q in range(PAIRS)], axis=1)

    qr0, qi0 = _lam_bar(es_ref)
    c_r, c_i = es_ref[3], es_ref[4]
    own_e = [(lid >> 4) == 2 * pr + (rid >> 6) for pr in range(PAIRS)]
    qr, qi = jnp.ones(shape, F32), jnp.zeros(shape, F32)
    for k in range(RUN + 1):
        er, ei = _cmul(c_r, c_i, qr, qi)
        for pr in range(PAIRS):
            e_scr[pr * LANES:(pr + 1) * LANES, k * LANES:(k + 1) * LANES] = jnp.where(own_e[pr], er, 0.0).astype(BF16)
            e_scr[(PAIRS + pr) * LANES:(PAIRS + pr + 1) * LANES, k * LANES:(k + 1) * LANES] = (
                jnp.where(own_e[pr], -ei, 0.0).astype(BF16))
        qr, qi = _cmul(qr, qi, qr0, qi0)

    kall = jnp.dot(f_scr[(RUN - 1) * LANES:, :], e_scr[:, :RUN * LANES], preferred_element_type=F32)
    kblk = [kall[:, k * LANES:(k + 1) * LANES] for k in range(RUN)]
    kblk[0] = kblk[0] + jnp.where(rid == lid, d_ref[...], 0.0)
    kblk = [kb.astype(BF16) for kb in kblk]
    zero = jnp.zeros(shape, BF16)
    for tau in range(RUN):
        for t in range(RUN):
            w_scr[tau * LANES:(tau + 1) * LANES, t * LANES:(t + 1) * LANES] = kblk[t - tau] if t >= tau else zero

    lhs = jnp.concatenate([u_ref[t] for t in range(RUN)], axis=1)

    lp = lhs[n_srow:]
    v_scr[...] = jnp.dot(lp, f_scr[...], preferred_element_type=F32)
    a_re, a_im = decay_row(p16[0]), decay_row(p16[1])
    n_run = n_seq * n_rpp

    def body(i, carry):
        out = []
        for n in range(n_seq):
            s_re, s_im = carry[n]
            row = n * n_rpp + i
            s_scr[pl.ds(row, 1), :sc] = s_re
            s_scr[pl.ds(row, 1), sc:] = s_im
            v = v_scr[pl.ds(row, 1), :]
            out.append((a_re * s_re - a_im * s_im + v[:, :sc],
                        a_re * s_im + a_im * s_re + v[:, sc:]))
        return tuple(out)

    s_scr[n_run:, :] = jnp.zeros((n_seq, 2 * sc), F32)
    init = tuple((v_scr[n_run + n:n_run + n + 1, :sc], v_scr[n_run + n:n_run + n + 1, sc:]) for n in range(n_seq))
    fin = lax.fori_loop(0, n_rpp, body, init)
    for n in range(n_seq):
        pre_ref[n:n + 1, :] = fin[n][0]
        pim_ref[n:n + 1, :] = fin[n][1]

    yp = jnp.dot(lp, w_scr[...], preferred_element_type=F32)
    yp = yp + jnp.dot(s_scr[...].astype(BF16), e_scr[:, LANES:], preferred_element_type=F32)
    for t in range(RUN):
        y_ref[t, n_srow:, :] = jax.nn.gelu(yp[:, t * LANES:(t + 1) * LANES]).astype(BF16)

    ls = lhs[:n_srow]
    ls2 = jnp.concatenate([ls[:, :hw], ls[:, hw:]], axis=0)
    s0_re = jnp.concatenate([sr0_ref[...], sr1_ref[...]], axis=0)
    s0_im = jnp.concatenate([si0_ref[...], si1_ref[...]], axis=0)
    s0 = jnp.concatenate([s0_re, s0_im], axis=1).astype(BF16)
    ys = jnp.dot(ls2, w_scr[:hw, :hw], preferred_element_type=F32)
    ys = ys + jnp.dot(s0, e_scr[:, LANES:LANES + hw], preferred_element_type=F32)
    vs = jnp.dot(ls2, f_scr[hw:, :], preferred_element_type=F32)
    b_re, b_im = decay_row(p8[0]), decay_row(p8[1])
    n_re = b_re * s0_re - b_im * s0_im + vs[:, :sc]
    n_im = b_re * s0_im + b_im * s0_re + vs[:, sc:]
    for h in range(2):
        sre_ref[h] = n_re[h * n_srow:(h + 1) * n_srow]
        sim_ref[h] = n_im[h * n_srow:(h + 1) * n_srow]
        for t in range(half):
            blk = ys[h * n_srow:(h + 1) * n_srow, t * LANES:(t + 1) * LANES]
            y_ref[t + half * h, :n_srow, :] = jax.nn.gelu(blk).astype(BF16)


def _conv_kernel(n_srow, n_seq, n_rpp,
                 ch_ref, cb_ref, cc_ref, w_ref, cs_ref, y_ref, so_ref, po_ref, zs_ref):
    pad = 8
    n_runs = ch_ref.shape[1]
    n_run = n_seq * n_rpp
    meta0 = n_srow + n_run
    w0 = w_ref[0:1, :]
    w1 = w_ref[1:2, :]
    w2 = w_ref[2:3, :]

    def gated(t):
        return cc_ref[t].astype(F32) * ch_ref[t].astype(F32)

    zs_ref[0, pad:pad + n_runs, :] = gated(RUN - 2)
    zs_ref[1, pad:pad + n_runs, :] = gated(RUN - 1)
    ridx = lax.broadcasted_iota(jnp.int32, (n_run, LANES), 0)
    lo = pad + n_srow - 1

    def previous(k):
        p = zs_ref[k, lo:lo + n_run, :]
        for n in range(n_seq):
            m = zs_ref[k, pad + meta0 + n:pad + meta0 + n + 1, :]
            p = jnp.where(ridx == n * n_rpp, m, p)
        return jnp.concatenate([p, jnp.zeros((n_seq, LANES), F32)], axis=0)

    p2, p1 = previous(0), previous(1)

    half = RUN // 2
    zm2 = zm1 = None
    for t in range(RUN):
        zt = gated(t)
        ztp, zts = zt[n_srow:], zt[:n_srow]
        if t == 0:
            a1, a2 = p1, p2
        elif t == 1:
            a1, a2 = zm1[n_srow:], p1
        else:
            a1, a2 = zm1[n_srow:], zm2[n_srow:]
        yp = cb_ref[t, n_srow:, :].astype(F32) * (w0 * a2 + w1 * a1 + w2 * ztp)
        y_ref[t, n_srow:, :] = yp.astype(BF16)
        h, tt = divmod(t, half)
        if tt == 0:
            b1, b2 = cs_ref[h, 1], cs_ref[h, 0]
        elif tt == 1:
            b1, b2 = zm1[:n_srow], cs_ref[h, 1]
        else:
            b1, b2 = zm1[:n_srow], zm2[:n_srow]
        ysm = cb_ref[t, :n_srow, :].astype(F32) * (w0 * b2 + w1 * b1 + w2 * zts)
        y_ref[t, :n_srow, :] = ysm.astype(BF16)
        if tt >= half - 2:
            so_ref[h, tt - (half - 2)] = zts
        zm2, zm1 = zm1, zt

    for k in range(2):
        for n in range(n_seq):
            row = pad + n_srow + (n + 1) * n_rpp - 1
            po_ref[k, n:n + 1, :] = zs_ref[k, row:row + 1, :]


def _merge_kernel(ys_ref, yc_ref, ga_ref, gb_ref, wv_ref, wg_ref, wc_ref, o_ref,
                  wvb_ref, wgb_ref, wcb_ref):
    @pl.when(pl.program_id(1) == 0)
    def _():
        wvb_ref[...] = wv_ref[...].astype(BF16)
        wgb_ref[...] = wg_ref[...].astype(BF16)
        wcb_ref[...] = wc_ref[...].astype(BF16)

    ys = ys_ref[...]
    y_a = (jnp.dot(ys, wvb_ref[...], preferred_element_type=F32)
           * jax.nn.sigmoid(jnp.dot(ys, wgb_ref[...], preferred_element_type=F32)))
    y_b = jnp.dot(yc_ref[...], wcb_ref[...], preferred_element_type=F32)
    m = (jax.nn.sigmoid(ga_ref[...].astype(F32)) * y_a
         + jax.nn.sigmoid(gb_ref[...].astype(F32)) * y_b)
    o_ref[...] = m.astype(BF16)


def _oproj_kernel(m_ref, w_ref, xs_ref, xp_ref, g1_ref, g2_ref, x1_ref, hf_ref):
    k = pl.program_id(1)

    @pl.when(k == 0)
    def _():
        x1_ref[...] = jnp.zeros_like(x1_ref)

    m = m_ref[...]
    for c0 in range(0, x1_ref.shape[1], COL_CHUNK):
        x1_ref[:, c0:c0 + COL_CHUNK] += jnp.dot(m, w_ref[:, c0:c0 + COL_CHUNK], preferred_element_type=F32)

    @pl.when(k == pl.num_programs(1) - 1)
    def _():
        g1 = g1_ref[...]
        g2 = g2_ref[...]
        ns = xs_ref.shape[0]
        pieces = [(0, ns, xs_ref, 0)] + [(ns + r0, nr, xp_ref, r0) for r0, nr in _row_chunks(xp_ref.shape[0])]
        for o0, nr, x_ref, r0 in pieces:
            x1 = x_ref[r0:r0 + nr, :] + _rms(x1_ref[o0:o0 + nr, :], g1)
            x1_ref[o0:o0 + nr, :] = x1
            hf_ref[o0:o0 + nr, :] = _rms(x1, g2).astype(BF16)


def _ffn_kernel(hf_ref, wg_ref, wu_ref, wd_ref, x1_ref, g_ref, ys_ref, yp_ref):
    j = pl.program_id(1)
    ns = ys_ref.shape[0]

    @pl.when(j == 0)
    def _():
        ys_ref[...] = jnp.zeros_like(ys_ref)
        yp_ref[...] = jnp.zeros_like(yp_ref)

    hf = hf_ref[...]
    gate = jnp.dot(hf, wg_ref[...], preferred_element_type=F32)
    up = jnp.dot(hf, wu_ref[...], preferred_element_type=F32)
    h = (jax.nn.silu(gate) * up).astype(BF16)
    for c0 in range(0, yp_ref.shape[1], COL_CHUNK):
        part = jnp.dot(h, wd_ref[:, c0:c0 + COL_CHUNK], preferred_element_type=F32)
        ys_ref[:, c0:c0 + COL_CHUNK] += part[:ns]
        yp_ref[:, c0:c0 + COL_CHUNK] += part[ns:]

    @pl.when(j == pl.num_programs(1) - 1)
    def _():
        g = g_ref[...]
        ys_ref[...] = x1_ref[0:ns, :] + _rms(ys_ref[...], g)
        for r0, nr in _row_chunks(yp_ref.shape[0]):
            yp_ref[r0:r0 + nr, :] = x1_ref[ns + r0:ns + r0 + nr, :] + _rms(yp_ref[r0:r0 + nr, :], g)


def _ssm_param_blocks(lam_re, lam_im, log_dt, b_re, b_im, c_re, c_im):
    g, p, hh, nb, gb = SSM_GROUPS, SSM_STATE, SSM_GROUP, N_BLOCKS, GROUPS_PER_BLOCK

    def rows_gh(a):
        a = jnp.broadcast_to(a[:, None, None, :], (g, hh, 2, p))
        return a.reshape(nb, LANES, LANES)

    def lanes_gh(a):
        a = jnp.transpose(a.reshape(nb, gb, p), (0, 2, 1))
        a = jnp.broadcast_to(a[:, None, :, :, None], (nb, 2, p, gb, hh))
        return a.reshape(nb, LANES, LANES)

    dt_gp = jnp.broadcast_to(log_dt[:, None], (g, p))
    bt = lambda a: jnp.broadcast_to(jnp.transpose(a, (0, 2, 1))[:, :, None, :], (g, hh, 2, p)).reshape(
        nb, LANES, LANES)

    def ct(a):
        a = jnp.transpose(a.reshape(nb, gb, hh, p), (0, 3, 1, 2))
        return jnp.broadcast_to(a[:, None], (nb, 2, p, gb, hh)).reshape(nb, LANES, LANES)

    fs = jnp.stack([rows_gh(lam_re), rows_gh(lam_im), rows_gh(dt_gp), bt(b_re), bt(b_im)], axis=1)
    es = jnp.stack([lanes_gh(lam_re), lanes_gh(lam_im), lanes_gh(dt_gp), ct(c_re), ct(c_im)], axis=1)
    return fs, es


def kernel(x_prompt, x_sample, state_ssm_re, state_ssm_im, state_conv, meta_tokens, g_pre_mix, w_in,
           ssm_lambda_re, ssm_lambda_im, ssm_log_dt, ssm_b_re, ssm_b_im, ssm_c_re, ssm_c_im, ssm_d,
           w_glu_v, w_glu_g, conv_w, w_conv_out, w_o, g_post_mix, g_pre_ffn, w_ffn_gate, w_ffn_up,
           w_ffn_down, g_post_ffn):
    nb_, seq, d = x_prompt.shape
    dec_b, dec_seq, _ = x_sample.shape
    assert d == D_MODEL and dec_seq == RUN // 2 and dec_b % 2 == 0
    assert seq % RUN == 0 and N_META == RUN and w_in.shape[0] == 1
    n_rpp = seq // RUN
    n_srow = dec_b // 2
    n_run = nb_ * n_rpp
    n_body = n_srow + n_run
    n_runs = n_body + nb_
    assert n_srow % 16 == 0 and n_run % 16 == 0

    xs_view = x_sample.reshape(n_srow, RUN * d)
    xp_view = x_prompt.reshape(n_run, RUN * d)

    row = lambda a: a.reshape(1, -1)
    plane = lambda r, c: (None, r, c)

    hn = pl.pallas_call(
        _prenorm_kernel,
        out_shape=jax.ShapeDtypeStruct((RUN, n_runs, d), BF16),
        grid=(RUN,),
        in_specs=[pl.BlockSpec((n_srow, d), lambda t: (0, t)),
                  pl.BlockSpec((n_run, d), lambda t: (0, t)),
                  pl.BlockSpec((N_META, d), lambda t: (0, 0)),
                  pl.BlockSpec((1, d), lambda t: (0, 0))],
        out_specs=pl.BlockSpec(plane(n_runs, d), lambda t: (t, 0, 0)),
        compiler_params=_params(("arbitrary",)),
        name="prenorm",
    )(xs_view, xp_view, meta_tokens, row(g_pre_mix))

    tn = 512
    proj = pl.pallas_call(
        _inproj_kernel,
        out_shape=jax.ShapeDtypeStruct((RUN, n_runs, IN_COLS), BF16),
        grid=(IN_COLS // tn, RUN),
        in_specs=[pl.BlockSpec(plane(n_runs, d), lambda j, t: (t, 0, 0)),
                  pl.BlockSpec((d, tn), lambda j, t: (0, j))],
        out_specs=pl.BlockSpec(plane(n_runs, tn), lambda j, t: (t, 0, j)),
        scratch_shapes=[pltpu.VMEM((d, tn), BF16)],
        compiler_params=_params(("arbitrary", "arbitrary")),
        name="inproj",
    )(hn, w_in.reshape(d, IN_COLS))

    fs, es = _ssm_param_blocks(ssm_lambda_re[0], ssm_lambda_im[0], ssm_log_dt[0], ssm_b_re[0], ssm_b_im[0],
                               ssm_c_re[0], ssm_c_im[0])
    sc = STATE_COLS
    st_re = state_ssm_re.reshape(n_srow, 2 * SSM_GROUPS * SSM_STATE)
    st_im = state_ssm_im.reshape(n_srow, 2 * SSM_GROUPS * SSM_STATE)
    n_prow = n_runs - n_srow
    st_spec = lambda h: pl.BlockSpec((n_srow, sc), lambda g: (0, h * N_BLOCKS + g))
    par_spec = pl.BlockSpec((None, 5, LANES, LANES), lambda g: (g, 0, 0, 0))
    y_ssm, s_re2, s_im2, p_re, p_im = pl.pallas_call(
        functools.partial(_ssm_kernel, n_srow, nb_, n_rpp),
        out_shape=(jax.ShapeDtypeStruct((RUN, n_runs, SSM_WIDTH), BF16),
                   jax.ShapeDtypeStruct((2, n_srow, SSM_GROUPS * SSM_STATE), F32),
                   jax.ShapeDtypeStruct((2, n_srow, SSM_GROUPS * SSM_STATE), F32),
                   jax.ShapeDtypeStruct((nb_, SSM_GROUPS * SSM_STATE), F32),
                   jax.ShapeDtypeStruct((nb_, SSM_GROUPS * SSM_STATE), F32)),
        grid=(N_BLOCKS,),
        in_specs=[pl.BlockSpec((RUN, n_runs, LANES), lambda g: (0, 0, g)),
                  par_spec, par_spec,
                  pl.BlockSpec((None, 1, LANES), lambda g: (g, 0, 0)),
                  st_spec(0), st_spec(1), st_spec(0), st_spec(1)],
        out_specs=(pl.BlockSpec((RUN, n_runs, LANES), lambda g: (0, 0, g)),
                   pl.BlockSpec((2, n_srow, sc), lambda g: (0, 0, g)),
                   pl.BlockSpec((2, n_srow, sc), lambda g: (0, 0, g)),
                   pl.BlockSpec((nb_, sc), lambda g: (0, g)),
                   pl.BlockSpec((nb_, sc), lambda g: (0, g))),
        scratch_shapes=[pltpu.VMEM((RUN * LANES, 2 * sc), BF16),
                        pltpu.VMEM((2 * sc, (RUN + 1) * LANES), BF16),
                        pltpu.VMEM((RUN * LANES, RUN * LANES), BF16),
                        pltpu.VMEM((n_prow, 2 * sc), F32),
                        pltpu.VMEM((n_prow, 2 * sc), F32)],
        compiler_params=_params(("arbitrary",)),
        name="s5",
    )(proj, fs, es, ssm_d.reshape(N_BLOCKS, 1, LANES), st_re, st_re, st_im, st_im)

    cblk = SSM_WIDTH // LANES
    cs = jnp.transpose(state_conv.reshape(n_srow, 2, CONV_K - 1, CONV_WIDTH), (1, 2, 0, 3))
    y_conv, cs_out, cp_out = pl.pallas_call(
        functools.partial(_conv_kernel, n_srow, nb_, n_rpp),
        out_shape=(jax.ShapeDtypeStruct((RUN, n_runs, CONV_WIDTH), BF16),
                   jax.ShapeDtypeStruct((2, CONV_K - 1, n_srow, CONV_WIDTH), F32),
                   jax.ShapeDtypeStruct((CONV_K - 1, nb_, CONV_WIDTH), F32)),
        grid=(CONV_WIDTH // LANES,),
        in_specs=[pl.BlockSpec((RUN, n_runs, LANES), lambda j: (0, 0, cblk + j)),
                  pl.BlockSpec((RUN, n_runs, LANES), lambda j: (0, 0, 2 * cblk + j)),
                  pl.BlockSpec((RUN, n_runs, LANES), lambda j: (0, 0, 3 * cblk + j)),
                  pl.BlockSpec((CONV_K, LANES), lambda j: (0, j)),
                  pl.BlockSpec((2, CONV_K - 1, n_srow, LANES), lambda j: (0, 0, 0, j))],
        out_specs=(pl.BlockSpec((RUN, n_runs, LANES), lambda j: (0, 0, j)),
                   pl.BlockSpec((2, CONV_K - 1, n_srow, LANES), lambda j: (0, 0, 0, j)),
                   pl.BlockSpec((CONV_K - 1, nb_, LANES), lambda j: (0, 0, j))),
        scratch_shapes=[pltpu.VMEM((2, n_runs + 16, LANES), F32)],
        compiler_params=_params(("arbitrary",)),
        name="shortconv",
    )(proj, proj, proj, conv_w.reshape(CONV_K, CONV_WIDTH), cs)

    tn3 = 512
    ga_blk = (SSM_WIDTH + 3 * CONV_WIDTH) // tn3
    gb_blk = ga_blk + d // tn3
    merged = pl.pallas_call(
        _merge_kernel,
        out_shape=jax.ShapeDtypeStruct((RUN, n_body, d), BF16),
        grid=(d // tn3, RUN),
        in_specs=[pl.BlockSpec(plane(n_body, SSM_WIDTH), lambda j, t: (t, 0, 0)),
                  pl.BlockSpec(plane(n_body, CONV_WIDTH), lambda j, t: (t, 0, 0)),
                  pl.BlockSpec(plane(n_body, tn3), lambda j, t: (t, 0, ga_blk + j)),
                  pl.BlockSpec(plane(n_body, tn3), lambda j, t: (t, 0, gb_blk + j)),
                  pl.BlockSpec((SSM_WIDTH, tn3), lambda j, t: (0, j)),
                  pl.BlockSpec((SSM_WIDTH, tn3), lambda j, t: (0, j)),
                  pl.BlockSpec((CONV_WIDTH, tn3), lambda j, t: (0, j))],
        out_specs=pl.BlockSpec(plane(n_body, tn3), lambda j, t: (t, 0, j)),
        scratch_shapes=[pltpu.VMEM((SSM_WIDTH, tn3), BF16),
                        pltpu.VMEM((SSM_WIDTH, tn3), BF16),
                        pltpu.VMEM((CONV_WIDTH, tn3), BF16)],
        compiler_params=_params(("arbitrary", "arbitrary")),
        name="merge",
    )(y_ssm, y_conv, proj, proj, w_glu_v.reshape(SSM_WIDTH, d), w_glu_g.reshape(SSM_WIDTH, d),
      w_conv_out.reshape(CONV_WIDTH, d))

    tk = 512
    x1, hf = pl.pallas_call(
        _oproj_kernel,
        out_shape=(jax.ShapeDtypeStruct((RUN, n_body, d), F32),
                   jax.ShapeDtypeStruct((RUN, n_body, d), BF16)),
        grid=(RUN, d // tk),
        in_specs=[pl.BlockSpec(plane(n_body, tk), lambda t, k: (t, 0, k)),
                  pl.BlockSpec((tk, d), lambda t, k: (k, 0)),
                  pl.BlockSpec((n_srow, d), lambda t, k: (0, t)),
                  pl.BlockSpec((n_run, d), lambda t, k: (0, t), pipeline_mode=pl.Buffered(1)),
                  pl.BlockSpec((1, d), lambda t, k: (0, 0)),
                  pl.BlockSpec((1, d), lambda t, k: (0, 0))],
        out_specs=(pl.BlockSpec(plane(n_body, d), lambda t, k: (t, 0, 0)),
                   pl.BlockSpec(plane(n_body, d), lambda t, k: (t, 0, 0))),
        compiler_params=_params(("arbitrary", "arbitrary")),
        name="oproj",
    )(merged, w_o.reshape(d, d).astype(BF16), xs_view, xp_view, row(g_post_mix), row(g_pre_ffn))

    tf = 256
    ys_view, yp_view = pl.pallas_call(
        _ffn_kernel,
        out_shape=(jax.ShapeDtypeStruct((n_srow, RUN * d), F32),
                   jax.ShapeDtypeStruct((n_run, RUN * d), F32)),
        grid=(RUN, D_FF // tf),
        in_specs=[pl.BlockSpec(plane(n_body, d), lambda t, j: (t, 0, 0)),
                  pl.BlockSpec((d, tf), lambda t, j: (0, j)),
                  pl.BlockSpec((d, tf), lambda t, j: (0, j)),
                  pl.BlockSpec((tf, d), lambda t, j: (j, 0)),
                  pl.BlockSpec(plane(n_body, d), lambda t, j: (t, 0, 0), pipeline_mode=pl.Buffered(1)),
                  pl.BlockSpec((1, d), lambda t, j: (0, 0))],
        out_specs=(pl.BlockSpec((n_srow, d), lambda t, j: (0, t)),
                   pl.BlockSpec((n_run, d), lambda t, j: (0, t))),
        compiler_params=_params(("arbitrary", "arbitrary")),
        name="ffn",
    )(hf, w_ffn_gate.reshape(d, D_FF).astype(BF16), w_ffn_up.reshape(d, D_FF).astype(BF16),
      w_ffn_down.reshape(D_FF, d).astype(BF16), x1, row(g_post_ffn))

    y_sample = ys_view.reshape(dec_b, dec_seq, d)
    y_prompt = yp_view.reshape(nb_, seq, d)

    def state_out(s):
        return jnp.transpose(s, (1, 0, 2)).reshape(1, dec_b, SSM_GROUPS, SSM_STATE)

    p_cv = jnp.transpose(cp_out, (1, 0, 2))[None]
    s_cv = jnp.transpose(cs_out, (2, 0, 1, 3)).reshape(1, dec_b, CONV_K - 1, CONV_WIDTH)
    return (y_prompt, y_sample,
            p_re.reshape(1, nb_, SSM_GROUPS, SSM_STATE), p_im.reshape(1, nb_, SSM_GROUPS, SSM_STATE), p_cv,
            state_out(s_re2), state_out(s_im2), s_cv)
```

```python
import functools

import jax
import jax.numpy as jnp
from jax import lax
from jax.experimental import pallas as pl
from jax.experimental.pallas import tpu as pltpu

F32 = jnp.float32
BF16 = jnp.bfloat16

D_MODEL = 4096
N_META = 16
SSM_WIDTH = 2048
SSM_GROUP = 16
SSM_GROUPS = 128
SSM_STATE = 64
CONV_WIDTH = 2048
CONV_K = 3
D_FF = 11008
IN_COLS = 16384
EPS = 1e-6

RUN = 16
LANES = 128
GROUPS_PER_BLOCK = LANES // SSM_GROUP
PAIRS = GROUPS_PER_BLOCK // 2
N_BLOCKS = SSM_WIDTH // LANES
STATE_COLS = GROUPS_PER_BLOCK * SSM_STATE
VMEM_LIMIT = 56 * 1024 * 1024
ROW_CHUNK = 128
COL_CHUNK = 512


def _params(semantics, limit=VMEM_LIMIT):
    return pltpu.CompilerParams(dimension_semantics=semantics, vmem_limit_bytes=limit)


def _rms(x, g):
    return x * lax.rsqrt(jnp.mean(x * x, axis=-1, keepdims=True) + EPS) * g


def _row_chunks(n):
    return [(s, min(ROW_CHUNK, n - s)) for s in range(0, n, ROW_CHUNK)]


def _cmul(ar, ai, br, bi):
    return ar * br - ai * bi, ar * bi + ai * br


def _prenorm_kernel(xs_ref, xp_ref, meta_ref, g_ref, o_ref):
    g = g_ref[...]
    ns, npr = xs_ref.shape[0], xp_ref.shape[0]
    o_ref[0:ns, :] = _rms(xs_ref[...], g).astype(BF16)
    for r0, nr in _row_chunks(npr):
        o_ref[ns + r0:ns + r0 + nr, :] = _rms(xp_ref[r0:r0 + nr, :], g).astype(BF16)
    m = _rms(meta_ref[pl.ds(pl.program_id(0), 1), :], g)
    n_meta_rows = o_ref.shape[0] - ns - npr
    o_ref[ns + npr:, :] = jnp.broadcast_to(m, (n_meta_rows, m.shape[1])).astype(BF16)


def _inproj_kernel(h_ref, w_ref, wg_ref, wu_ref, wd_ref, wo_ref,
                   o_ref, wgb_ref, wub_ref, wdb_ref, wob_ref, wb_ref):
    @pl.when(pl.program_id(1) == 0)
    def _():
        wb_ref[...] = w_ref[...].astype(BF16)

    o_ref[...] = jnp.dot(h_ref[...], wb_ref[...], preferred_element_type=F32).astype(BF16)
    wgb_ref[...] = wg_ref[...].astype(BF16)
    wub_ref[...] = wu_ref[...].astype(BF16)
    wdb_ref[...] = wd_ref[...].astype(BF16)
    wob_ref[...] = wo_ref[...].astype(BF16)


def _lam_bar(p_ref):
    dt = jnp.exp(p_ref[2])
    mag = jnp.exp(p_ref[0] * dt)
    ang = p_ref[1] * dt
    return mag * jnp.cos(ang), mag * jnp.sin(ang)


def _ssm_kernel(n_srow, n_seq, n_rpp,
                u_ref, fs_ref, es_ref, d_ref, sr0_ref, sr1_ref, si0_ref, si1_ref,
                y_ref, sre_ref, sim_ref, pre_ref, pim_ref,
                f_scr, e_scr, w_scr, v_scr, s_scr):
    half = RUN // 2
    hw = half * LANES
    sc = STATE_COLS
    shape = (LANES, LANES)
    rid = lax.broadcasted_iota(jnp.int32, shape, 0)
    lid = lax.broadcasted_iota(jnp.int32, shape, 1)

    lr, li = _lam_bar(fs_ref)
    a, b = fs_ref[0], fs_ref[1]
    inv = 1.0 / (a * a + b * b)
    cr = ((lr - 1.0) * a + li * b) * inv
    ci = (li * a - (lr - 1.0) * b) * inv
    bbr, bbi = _cmul(cr, ci, fs_ref[3], fs_ref[4])
    own_f = [(rid >> 4) == 2 * pr + (lid >> 6) for pr in range(PAIRS)]
    pr_, pi_ = jnp.ones(shape, F32), jnp.zeros(shape, F32)
    p8 = None
    for k in range(RUN):
        if k == half:
            p8 = (pr_, pi_)
        fr, fi = _cmul(pr_, pi_, bbr, bbi)
        r0 = (RUN - 1 - k) * LANES
        for pr in range(PAIRS):
            f_scr[r0:r0 + LANES, pr * LANES:(pr + 1) * LANES] = jnp.where(own_f[pr], fr, 0.0).astype(BF16)
            f_scr[r0:r0 + LANES, (PAIRS + pr) * LANES:(PAIRS + pr + 1) * LANES] = (
                jnp.where(own_f[pr], fi, 0.0).astype(BF16))
        pr_, pi_ = _cmul(pr_, pi_, lr, li)
    p16 = (pr_, pi_)

    def decay_row(p):
        lo_half = lid[0:1, :] < SSM_STATE
        return jnp.concatenate(
            [jnp.where(lo_half, p[2 * q * SSM_GROUP:2 * q * SSM_GROUP + 1, :],
                       p[(2 * q + 1) * SSM_GROUP:(2 * q + 1) * SSM_GROUP + 1, :]) for q in range(PAIRS)], axis=1)

    qr0, qi0 = _lam_bar(es_ref)
    c_r, c_i = es_ref[3], es_ref[4]
    own_e = [(lid >> 4) == 2 * pr + (rid >> 6) for pr in range(PAIRS)]
    qr, qi = jnp.ones(shape, F32), jnp.zeros(shape, F32)
    for k in range(RUN + 1):
        er, ei = _cmul(c_r, c_i, qr, qi)
        for pr in range(PAIRS):
            e_scr[pr * LANES:(pr + 1) * LANES, k * LANES:(k + 1) * LANES] = jnp.where(own_e[pr], er, 0.0).astype(BF16)
            e_scr[(PAIRS + pr) * LANES:(PAIRS + pr + 1) * LANES, k * LANES:(k + 1) * LANES] = (
                jnp.where(own_e[pr], -ei, 0.0).astype(BF16))
        qr, qi = _cmul(qr, qi, qr0, qi0)

    kall = jnp.dot(f_scr[(RUN - 1) * LANES:, :], e_scr[:, :RUN * LANES], preferred_element_type=F32)
    kblk = [kall[:, k * LANES:(k + 1) * LANES] for k in range(RUN)]
    kblk[0] = kblk[0] + jnp.where(rid == lid, d_ref[...], 0.0)
    kblk = [kb.astype(BF16) for kb in kblk]
    zero = jnp.zeros(shape, BF16)
    for t in range(RUN):
        for tau in range((t | 1) + 1):
            w_scr[tau * LANES:(tau + 1) * LANES, t * LANES:(t + 1) * LANES] = kblk[t - tau] if t >= tau else zero

    def run_outputs(lhs_rows, state_rows, m):
        kdim, c0 = 2 * (m + 1) * LANES, 2 * m * LANES
        y2 = jnp.dot(lhs_rows[:, :kdim], w_scr[:kdim, c0:c0 + 2 * LANES], preferred_element_type=F32)
        return y2 + jnp.dot(state_rows, e_scr[:, LANES + c0:3 * LANES + c0], preferred_element_type=F32)

    lhs = jnp.concatenate([u_ref[t] for t in range(RUN)], axis=1)

    lp = lhs[n_srow:]
    v_scr[...] = jnp.dot(lp, f_scr[...], preferred_element_type=F32)
    a_re, a_im = decay_row(p16[0]), decay_row(p16[1])
    n_run = n_seq * n_rpp

    s_scr[n_run:, :] = jnp.zeros((n_seq, 2 * sc), F32)
    for n in range(n_seq):
        s_re, s_im = v_scr[n_run + n:n_run + n + 1, :sc], v_scr[n_run + n:n_run + n + 1, sc:]
        for i in range(n_rpp):
            r = n * n_rpp + i
            s_scr[r:r + 1, :sc] = s_re
            s_scr[r:r + 1, sc:] = s_im
            s_re, s_im = (a_re * s_re - a_im * s_im + v_scr[r:r + 1, :sc],
                          a_re * s_im + a_im * s_re + v_scr[r:r + 1, sc:])
        pre_ref[n:n + 1, :] = s_re
        pim_ref[n:n + 1, :] = s_im

    sp = s_scr[...].astype(BF16)
    for m in range(half):
        y2 = jax.nn.gelu(run_outputs(lp, sp, m)).astype(BF16)
        y_ref[2 * m, n_srow:, :] = y2[:, :LANES]
        y_ref[2 * m + 1, n_srow:, :] = y2[:, LANES:]

    ls = lhs[:n_srow]
    ls2 = jnp.concatenate([ls[:, :hw], ls[:, hw:]], axis=0)
    s0_re = jnp.concatenate([sr0_ref[...], sr1_ref[...]], axis=0)
    s0_im = jnp.concatenate([si0_ref[...], si1_ref[...]], axis=0)
    s0 = jnp.concatenate([s0_re, s0_im], axis=1).astype(BF16)
    vs = jnp.dot(ls2, f_scr[hw:, :], preferred_element_type=F32)
    b_re, b_im = decay_row(p8[0]), decay_row(p8[1])
    n_re = b_re * s0_re - b_im * s0_im + vs[:, :sc]
    n_im = b_re * s0_im + b_im * s0_re + vs[:, sc:]
    for h in range(2):
        sre_ref[h] = n_re[h * n_srow:(h + 1) * n_srow]
        sim_ref[h] = n_im[h * n_srow:(h + 1) * n_srow]
    for m in range(half // 2):
        y2 = jax.nn.gelu(run_outputs(ls2, s0, m)).astype(BF16)
        for h in range(2):
            y_ref[2 * m + half * h, :n_srow, :] = y2[h * n_srow:(h + 1) * n_srow, :LANES]
            y_ref[2 * m + 1 + half * h, :n_srow, :] = y2[h * n_srow:(h + 1) * n_srow, LANES:]


def _conv_kernel(n_srow, n_seq, n_rpp,
                 ch_ref, cb_ref, cc_ref, w_ref, cs_ref, y_ref, so_ref, po_ref, zs_ref):
    pad = 8
    n_runs = ch_ref.shape[1]
    n_run = n_seq * n_rpp
    meta0 = n_srow + n_run
    w0 = w_ref[0:1, :]
    w1 = w_ref[1:2, :]
    w2 = w_ref[2:3, :]

    def gated(t):
        return cc_ref[t].astype(F32) * ch_ref[t].astype(F32)

    zs_ref[0, pad:pad + n_runs, :] = gated(RUN - 2)
    zs_ref[1, pad:pad + n_runs, :] = gated(RUN - 1)
    ridx = lax.broadcasted_iota(jnp.int32, (n_run, LANES), 0)
    lo = pad + n_srow - 1

    def previous(k):
        p = zs_ref[k, lo:lo + n_run, :]
        for n in range(n_seq):
            m = zs_ref[k, pad + meta0 + n:pad + meta0 + n + 1, :]
            p = jnp.where(ridx == n * n_rpp, m, p)
        return jnp.concatenate([p, jnp.zeros((n_seq, LANES), F32)], axis=0)

    p2, p1 = previous(0), previous(1)

    half = RUN // 2
    zm2 = zm1 = None
    for t in range(RUN):
        zt = gated(t)
        ztp, zts = zt[n_srow:], zt[:n_srow]
        if t == 0:
            a1, a2 = p1, p2
        elif t == 1:
            a1, a2 = zm1[n_srow:], p1
        else:
            a1, a2 = zm1[n_srow:], zm2[n_srow:]
        yp = cb_ref[t, n_srow:, :].astype(F32) * (w0 * a2 + w1 * a1 + w2 * ztp)
        y_ref[t, n_srow:, :] = yp.astype(BF16)
        h, tt = divmod(t, half)
        if tt == 0:
            b1, b2 = cs_ref[h, 1], cs_ref[h, 0]
        elif tt == 1:
            b1, b2 = zm1[:n_srow], cs_ref[h, 1]
        else:
            b1, b2 = zm1[:n_srow], zm2[:n_srow]
        ysm = cb_ref[t, :n_srow, :].astype(F32) * (w0 * b2 + w1 * b1 + w2 * zts)
        y_ref[t, :n_srow, :] = ysm.astype(BF16)
        if tt >= half - 2:
            so_ref[h, tt - (half - 2)] = zts
        zm2, zm1 = zm1, zt

    for k in range(2):
        for n in range(n_seq):
            row = pad + n_srow + (n + 1) * n_rpp - 1
            po_ref[k, n:n + 1, :] = zs_ref[k, row:row + 1, :]


def _merge_kernel(ys_ref, yc_ref, ga_ref, gb_ref, wv_ref, wg_ref, wc_ref, o_ref,
                  wvb_ref, wgb_ref, wcb_ref):
    @pl.when(pl.program_id(1) == 0)
    def _():
        wvb_ref[...] = wv_ref[...].astype(BF16)
        wgb_ref[...] = wg_ref[...].astype(BF16)
        wcb_ref[...] = wc_ref[...].astype(BF16)

    ys = ys_ref[...]
    y_a = (jnp.dot(ys, wvb_ref[...], preferred_element_type=F32)
           * jax.nn.sigmoid(jnp.dot(ys, wgb_ref[...], preferred_element_type=F32)))
    y_b = jnp.dot(yc_ref[...], wcb_ref[...], preferred_element_type=F32)
    m = (jax.nn.sigmoid(ga_ref[...].astype(F32)) * y_a
         + jax.nn.sigmoid(gb_ref[...].astype(F32)) * y_b)
    o_ref[...] = m.astype(BF16)


def _oproj_kernel(m_ref, w_ref, xs_ref, xp_ref, g1_ref, g2_ref, x1_ref, hf_ref):
    k = pl.program_id(1)

    @pl.when(k == 0)
    def _():
        x1_ref[...] = jnp.zeros_like(x1_ref)

    m = m_ref[...]
    for c0 in range(0, x1_ref.shape[1], COL_CHUNK):
        x1_ref[:, c0:c0 + COL_CHUNK] += jnp.dot(m, w_ref[:, c0:c0 + COL_CHUNK], preferred_element_type=F32)

    @pl.when(k == pl.num_programs(1) - 1)
    def _():
        g1 = g1_ref[...]
        g2 = g2_ref[...]
        ns = xs_ref.shape[0]
        pieces = [(0, ns, xs_ref, 0)] + [(ns + r0, nr, xp_ref, r0) for r0, nr in _row_chunks(xp_ref.shape[0])]
        for o0, nr, x_ref, r0 in pieces:
            x1 = x_ref[r0:r0 + nr, :] + _rms(x1_ref[o0:o0 + nr, :], g1)
            x1_ref[o0:o0 + nr, :] = x1
            hf_ref[o0:o0 + nr, :] = _rms(x1, g2).astype(BF16)


def _ffn_kernel(hf_ref, wg_ref, wu_ref, wd_ref, x1_ref, g_ref, ys_ref, yp_ref):
    j = pl.program_id(1)
    ns = ys_ref.shape[0]

    @pl.when(j == 0)
    def _():
        ys_ref[...] = jnp.zeros_like(ys_ref)
        yp_ref[...] = jnp.zeros_like(yp_ref)

    hf = hf_ref[...]
    gate = jnp.dot(hf, wg_ref[...], preferred_element_type=F32)
    up = jnp.dot(hf, wu_ref[...], preferred_element_type=F32)
    h = (jax.nn.silu(gate) * up).astype(BF16)
    for c0 in range(0, yp_ref.shape[1], COL_CHUNK):
        part = jnp.dot(h, wd_ref[:, c0:c0 + COL_CHUNK], preferred_element_type=F32)
        ys_ref[:, c0:c0 + COL_CHUNK] += part[:ns]
        yp_ref[:, c0:c0 + COL_CHUNK] += part[ns:]

    @pl.when(j == pl.num_programs(1) - 1)
    def _():
        g = g_ref[...]
        ys_ref[...] = x1_ref[0:ns, :] + _rms(ys_ref[...], g)
        for r0, nr in _row_chunks(yp_ref.shape[0]):
            yp_ref[r0:r0 + nr, :] = x1_ref[ns + r0:ns + r0 + nr, :] + _rms(yp_ref[r0:r0 + nr, :], g)


def _ssm_param_blocks(lam_re, lam_im, log_dt, b_re, b_im, c_re, c_im):
    g, p, hh, nb, gb = SSM_GROUPS, SSM_STATE, SSM_GROUP, N_BLOCKS, GROUPS_PER_BLOCK

    def rows_gh(a):
        a = jnp.broadcast_to(a[:, None, None, :], (g, hh, 2, p))
        return a.reshape(nb, LANES, LANES)

    def lanes_gh(a):
        a = jnp.transpose(a.reshape(nb, gb, p), (0, 2, 1))
        a = jnp.broadcast_to(a[:, None, :, :, None], (nb, 2, p, gb, hh))
        return a.reshape(nb, LANES, LANES)

    dt_gp = jnp.broadcast_to(log_dt[:, None], (g, p))
    bt = lambda a: jnp.broadcast_to(jnp.transpose(a, (0, 2, 1))[:, :, None, :], (g, hh, 2, p)).reshape(
        nb, LANES, LANES)

    def ct(a):
        a = jnp.transpose(a.reshape(nb, gb, hh, p), (0, 3, 1, 2))
        return jnp.broadcast_to(a[:, None], (nb, 2, p, gb, hh)).reshape(nb, LANES, LANES)

    fs = jnp.stack([rows_gh(lam_re), rows_gh(lam_im), rows_gh(dt_gp), bt(b_re), bt(b_im)], axis=1)
    es = jnp.stack([lanes_gh(lam_re), lanes_gh(lam_im), lanes_gh(dt_gp), ct(c_re), ct(c_im)], axis=1)
    return fs, es


def kernel(x_prompt, x_sample, state_ssm_re, state_ssm_im, state_conv, meta_tokens, g_pre_mix, w_in,
           ssm_lambda_re, ssm_lambda_im, ssm_log_dt, ssm_b_re, ssm_b_im, ssm_c_re, ssm_c_im, ssm_d,
           w_glu_v, w_glu_g, conv_w, w_conv_out, w_o, g_post_mix, g_pre_ffn, w_ffn_gate, w_ffn_up,
           w_ffn_down, g_post_ffn):
    nb_, seq, d = x_prompt.shape
    dec_b, dec_seq, _ = x_sample.shape
    assert d == D_MODEL and dec_seq == RUN // 2 and dec_b % 2 == 0
    assert seq % RUN == 0 and N_META == RUN and w_in.shape[0] == 1
    n_rpp = seq // RUN
    n_srow = dec_b // 2
    n_run = nb_ * n_rpp
    n_body = n_srow + n_run
    n_runs = n_body + nb_
    assert n_srow % 16 == 0 and n_run % 16 == 0

    xs_view = x_sample.reshape(n_srow, RUN * d)
    xp_view = x_prompt.reshape(n_run, RUN * d)

    row = lambda a: a.reshape(1, -1)
    plane = lambda r, c: (None, r, c)

    hn = pl.pallas_call(
        _prenorm_kernel,
        out_shape=jax.ShapeDtypeStruct((RUN, n_runs, d), BF16),
        grid=(RUN,),
        in_specs=[pl.BlockSpec((n_srow, d), lambda t: (0, t)),
                  pl.BlockSpec((n_run, d), lambda t: (0, t)),
                  pl.BlockSpec((N_META, d), lambda t: (0, 0)),
                  pl.BlockSpec((1, d), lambda t: (0, 0))],
        out_specs=pl.BlockSpec(plane(n_runs, d), lambda t: (t, 0, 0)),
        compiler_params=_params(("arbitrary",)),
        name="prenorm",
    )(xs_view, xp_view, meta_tokens, row(g_pre_mix))

    tn = 512
    n_steps = (IN_COLS // tn) * RUN
    slab = 2 * d // n_steps
    slab_d = 32
    n_slab_d = D_FF // slab_d
    assert slab * n_steps == 2 * d and slab % 16 == 0 and n_slab_d * slab_d == D_FF and n_slab_d <= n_steps
    pair_map = lambda j, t: ((j * RUN + t) // 2, 0)
    down_map = lambda j, t: (jnp.minimum(j * RUN + t, n_slab_d - 1), 0)
    proj, wg_b, wu_b, wd_b, wo_b = pl.pallas_call(
        _inproj_kernel,
        out_shape=(jax.ShapeDtypeStruct((RUN, n_runs, IN_COLS), BF16),
                   jax.ShapeDtypeStruct((d, D_FF), BF16),
                   jax.ShapeDtypeStruct((d, D_FF), BF16),
                   jax.ShapeDtypeStruct((D_FF, d), BF16),
                   jax.ShapeDtypeStruct((d, d), BF16)),
        grid=(IN_COLS // tn, RUN),
        in_specs=[pl.BlockSpec(plane(n_runs, d), lambda j, t: (t, 0, 0)),
                  pl.BlockSpec((d, tn), lambda j, t: (0, j)),
                  pl.BlockSpec((slab, D_FF), pair_map),
                  pl.BlockSpec((slab, D_FF), pair_map),
                  pl.BlockSpec((slab_d, d), down_map),
                  pl.BlockSpec((slab, d), pair_map)],
        out_specs=(pl.BlockSpec(plane(n_runs, tn), lambda j, t: (t, 0, j)),
                   pl.BlockSpec((slab, D_FF), pair_map),
                   pl.BlockSpec((slab, D_FF), pair_map),
                   pl.BlockSpec((slab_d, d), down_map),
                   pl.BlockSpec((slab, d), pair_map)),
        scratch_shapes=[pltpu.VMEM((d, tn), BF16)],
        compiler_params=_params(("arbitrary", "arbitrary")),
        name="inproj",
    )(hn, w_in.reshape(d, IN_COLS), w_ffn_gate.reshape(d, D_FF), w_ffn_up.reshape(d, D_FF),
      w_ffn_down.reshape(D_FF, d), w_o.reshape(d, d))

    fs, es = _ssm_param_blocks(ssm_lambda_re[0], ssm_lambda_im[0], ssm_log_dt[0], ssm_b_re[0], ssm_b_im[0],
                               ssm_c_re[0], ssm_c_im[0])
    sc = STATE_COLS
    st_re = state_ssm_re.reshape(n_srow, 2 * SSM_GROUPS * SSM_STATE)
    st_im = state_ssm_im.reshape(n_srow, 2 * SSM_GROUPS * SSM_STATE)
    n_prow = n_runs - n_srow
    st_spec = lambda h: pl.BlockSpec((n_srow, sc), lambda g: (0, h * N_BLOCKS + g))
    par_spec = pl.BlockSpec((None, 5, LANES, LANES), lambda g: (g, 0, 0, 0))
    y_ssm, s_re2, s_im2, p_re, p_im = pl.pallas_call(
        functools.partial(_ssm_kernel, n_srow, nb_, n_rpp),
        out_shape=(jax.ShapeDtypeStruct((RUN, n_runs, SSM_WIDTH), BF16),
                   jax.ShapeDtypeStruct((2, n_srow, SSM_GROUPS * SSM_STATE), F32),
                   jax.ShapeDtypeStruct((2, n_srow, SSM_GROUPS * SSM_STATE), F32),
                   jax.ShapeDtypeStruct((nb_, SSM_GROUPS * SSM_STATE), F32),
                   jax.ShapeDtypeStruct((nb_, SSM_GROUPS * SSM_STATE), F32)),
        grid=(N_BLOCKS,),
        in_specs=[pl.BlockSpec((RUN, n_runs, LANES), lambda g: (0, 0, g)),
                  par_spec, par_spec,
                  pl.BlockSpec((None, 1, LANES), lambda g: (g, 0, 0)),
                  st_spec(0), st_spec(1), st_spec(0), st_spec(1)],
        out_specs=(pl.BlockSpec((RUN, n_runs, LANES), lambda g: (0, 0, g)),
                   pl.BlockSpec((2, n_srow, sc), lambda g: (0, 0, g)),
                   pl.BlockSpec((2, n_srow, sc), lambda g: (0, 0, g)),
                   pl.BlockSpec((nb_, sc), lambda g: (0, g)),
                   pl.BlockSpec((nb_, sc), lambda g: (0, g))),
        scratch_shapes=[pltpu.VMEM((RUN * LANES, 2 * sc), BF16),
                        pltpu.VMEM((2 * sc, (RUN + 1) * LANES), BF16),
                        pltpu.VMEM((RUN * LANES, RUN * LANES), BF16),
                        pltpu.VMEM((n_prow, 2 * sc), F32),
                        pltpu.VMEM((n_prow, 2 * sc), F32)],
        compiler_params=_params(("arbitrary",)),
        name="s5",
    )(proj, fs, es, ssm_d.reshape(N_BLOCKS, 1, LANES), st_re, st_re, st_im, st_im)

    cblk = SSM_WIDTH // LANES
    cs = jnp.transpose(state_conv.reshape(n_srow, 2, CONV_K - 1, CONV_WIDTH), (1, 2, 0, 3))
    y_conv, cs_out, cp_out = pl.pallas_call(
        functools.partial(_conv_kernel, n_srow, nb_, n_rpp),
        out_shape=(jax.ShapeDtypeStruct((RUN, n_runs, CONV_WIDTH), BF16),
                   jax.ShapeDtypeStruct((2, CONV_K - 1, n_srow, CONV_WIDTH), F32),
                   jax.ShapeDtypeStruct((CONV_K - 1, nb_, CONV_WIDTH), F32)),
        grid=(CONV_WIDTH // LANES,),
        in_specs=[pl.BlockSpec((RUN, n_runs, LANES), lambda j: (0, 0, cblk + j)),
                  pl.BlockSpec((RUN, n_runs, LANES), lambda j: (0, 0, 2 * cblk + j)),
                  pl.BlockSpec((RUN, n_runs, LANES), lambda j: (0, 0, 3 * cblk + j)),
                  pl.BlockSpec((CONV_K, LANES), lambda j: (0, j)),
                  pl.BlockSpec((2, CONV_K - 1, n_srow, LANES), lambda j: (0, 0, 0, j))],
        out_specs=(pl.BlockSpec((RUN, n_runs, LANES), lambda j: (0, 0, j)),
                   pl.BlockSpec((2, CONV_K - 1, n_srow, LANES), lambda j: (0, 0, 0, j)),
                   pl.BlockSpec((CONV_K - 1, nb_, LANES), lambda j: (0, 0, j))),
        scratch_shapes=[pltpu.VMEM((2, n_runs + 16, LANES), F32)],
        compiler_params=_params(("arbitrary",)),
        name="shortconv",
    )(proj, proj, proj, conv_w.reshape(CONV_K, CONV_WIDTH), cs)

    tn3 = 512
    ga_blk = (SSM_WIDTH + 3 * CONV_WIDTH) // tn3
    gb_blk = ga_blk + d // tn3
    merged = pl.pallas_call(
        _merge_kernel,
        out_shape=jax.ShapeDtypeStruct((RUN, n_body, d), BF16),
        grid=(d // tn3, RUN),
        in_specs=[pl.BlockSpec(plane(n_body, SSM_WIDTH), lambda j, t: (t, 0, 0)),
                  pl.BlockSpec(plane(n_body, CONV_WIDTH), lambda j, t: (t, 0, 0)),
                  pl.BlockSpec(plane(n_body, tn3), lambda j, t: (t, 0, ga_blk + j)),
                  pl.BlockSpec(plane(n_body, tn3), lambda j, t: (t, 0, gb_blk + j)),
                  pl.BlockSpec((SSM_WIDTH, tn3), lambda j, t: (0, j)),
                  pl.BlockSpec((SSM_WIDTH, tn3), lambda j, t: (0, j)),
                  pl.BlockSpec((CONV_WIDTH, tn3), lambda j, t: (0, j))],
        out_specs=pl.BlockSpec(plane(n_body, tn3), lambda j, t: (t, 0, j)),
        scratch_shapes=[pltpu.VMEM((SSM_WIDTH, tn3), BF16),
                        pltpu.VMEM((SSM_WIDTH, tn3), BF16),
                        pltpu.VMEM((CONV_WIDTH, tn3), BF16)],
        compiler_params=_params(("arbitrary", "arbitrary")),
        name="merge",
    )(y_ssm, y_conv, proj, proj, w_glu_v.reshape(SSM_WIDTH, d), w_glu_g.reshape(SSM_WIDTH, d),
      w_conv_out.reshape(CONV_WIDTH, d))

    tk = 512
    x1, hf = pl.pallas_call(
        _oproj_kernel,
        out_shape=(jax.ShapeDtypeStruct((RUN, n_body, d), F32),
                   jax.ShapeDtypeStruct((RUN, n_body, d), BF16)),
        grid=(RUN, d // tk),
        in_specs=[pl.BlockSpec(plane(n_body, tk), lambda t, k: (t, 0, k)),
                  pl.BlockSpec((tk, d), lambda t, k: (k, 0)),
                  pl.BlockSpec((n_srow, d), lambda t, k: (0, t)),
                  pl.BlockSpec((n_run, d), lambda t, k: (0, t), pipeline_mode=pl.Buffered(1)),
                  pl.BlockSpec((1, d), lambda t, k: (0, 0)),
                  pl.BlockSpec((1, d), lambda t, k: (0, 0))],
        out_specs=(pl.BlockSpec(plane(n_body, d), lambda t, k: (t, 0, 0)),
                   pl.BlockSpec(plane(n_body, d), lambda t, k: (t, 0, 0))),
        compiler_params=_params(("arbitrary", "arbitrary")),
        name="oproj",
    )(merged, wo_b, xs_view, xp_view, row(g_post_mix), row(g_pre_ffn))

    tf = 256
    ys_view, yp_view = pl.pallas_call(
        _ffn_kernel,
        out_shape=(jax.ShapeDtypeStruct((n_srow, RUN * d), F32),
                   jax.ShapeDtypeStruct((n_run, RUN * d), F32)),
        grid=(RUN, D_FF // tf),
        in_specs=[pl.BlockSpec(plane(n_body, d), lambda t, j: (t, 0, 0)),
                  pl.BlockSpec((d, tf), lambda t, j: (0, j)),
                  pl.BlockSpec((d, tf), lambda t, j: (0, j)),
                  pl.BlockSpec((tf, d), lambda t, j: (j, 0)),
                  pl.BlockSpec(plane(n_body, d), lambda t, j: (t, 0, 0), pipeline_mode=pl.Buffered(1)),
                  pl.BlockSpec((1, d), lambda t, j: (0, 0))],
        out_specs=(pl.BlockSpec((n_srow, d), lambda t, j: (0, t)),
                   pl.BlockSpec((n_run, d), lambda t, j: (0, t))),
        compiler_params=_params(("arbitrary", "arbitrary")),
        name="ffn",
    )(hf, wg_b, wu_b, wd_b, x1, row(g_post_ffn))

    y_sample = ys_view.reshape(dec_b, dec_seq, d)
    y_prompt = yp_view.reshape(nb_, seq, d)

    def state_out(s):
        return jnp.transpose(s, (1, 0, 2)).reshape(1, dec_b, SSM_GROUPS, SSM_STATE)

    p_cv = jnp.transpose(cp_out, (1, 0, 2))[None]
    s_cv = jnp.transpose(cs_out, (2, 0, 1, 3)).reshape(1, dec_b, CONV_K - 1, CONV_WIDTH)
    return (y_prompt, y_sample,
            p_re.reshape(1, nb_, SSM_GROUPS, SSM_STATE), p_im.reshape(1, nb_, SSM_GROUPS, SSM_STATE), p_cv,
            state_out(s_re2), state_out(s_im2), s_cv)
```
